```python
import jax, jax.numpy as jnp
from jax import lax
import numpy as np

D_MODEL = 1024
BATCH = 8
SEQ = 2048
DEPTH = 4

N_A_LAYERS = DEPTH // 2
N_B_LAYERS = DEPTH - N_A_LAYERS
EPS = 1e-5
N_MOD = 9
D_FF = 2816
FFN_HALF = 0.5
SSM_EXPAND = 2
D_INNER = SSM_EXPAND * D_MODEL
SSM_HEADDIM = 64
SSM_HEADS = D_INNER // SSM_HEADDIM
SSM_GROUPS = 8
SSM_STATE = 128
CONV_WIDTH = 4
CHUNK = 128
CONV_DIM = D_INNER + 2 * SSM_GROUPS * SSM_STATE
IN_PROJ_DIM = D_INNER + CONV_DIM + SSM_HEADS
ATT_HEADS = 16
KV_HEADS = 4
HEAD_DIM = 64
WINDOW = 128

kernel_name = "yoco_mamba2_swa_sink_macaron_adaln"


def rmsnorm(x, w):
    xf = x.astype(jnp.float32)
    y = xf * lax.rsqrt(jnp.mean(xf * xf, axis=-1, keepdims=True) + EPS)
    return (y * w.astype(jnp.float32)).astype(x.dtype)


def modulate(h, shift, scale):
    return h * (1 + scale[:, None, :]) + shift[:, None, :]


def swiglu(h, w_gu, w_down):
    g, u = jnp.split(h @ w_gu, 2, axis=-1)
    return (jax.nn.silu(g) * u) @ w_down


def causal_dwconv(u, w, b):
    y = lax.conv_general_dilated(u, w[:, None, :], window_strides=(1,),
                                 padding=[(CONV_WIDTH - 1, 0)],
                                 dimension_numbers=("NWC", "WIO", "NWC"),
                                 feature_group_count=u.shape[-1])
    return y + b


def ssd_scan(x, dt, A, Bm, Cm):
    b, l, h, p = x.shape
    g, n = Bm.shape[-2:]
    r = h // g
    nc = l // CHUNK
    X = (x * dt[..., None]).reshape(b, nc, CHUNK, g, r, p)
    a = jnp.moveaxis((dt * A).reshape(b, nc, CHUNK, g, r), 2, -1)
    a_cs = jnp.cumsum(a, axis=-1)
    Bc = Bm.reshape(b, nc, CHUNK, g, n)
    Cc = Cm.reshape(b, nc, CHUNK, g, n)
    idx = jnp.arange(CHUNK)
    causal = idx[:, None] >= idx[None, :]
    seg = a_cs[..., :, None] - a_cs[..., None, :]
    Lmat = jnp.exp(jnp.where(causal, seg, -jnp.inf))
    CB = jnp.einsum("bclgn,bcsgn->bcgls", Cc, Bc)
    y_diag = jnp.einsum("bcgrls,bcsgrp->bclgrp", CB[:, :, :, None] * Lmat, X)
    decay_to_end = jnp.exp(a_cs[..., -1:] - a_cs)
    chunk_states = jnp.einsum("bclgn,bcgrl,bclgrp->bcgrpn", Bc, decay_to_end, X)
    chunk_decay = jnp.exp(a_cs[..., -1])

    def step(state, inp):
        s_c, d_c = inp
        return state * d_c[..., None, None] + s_c, state

    init = jnp.zeros((b, g, r, p, n), jnp.float32)
    _, prev = lax.scan(step, init, (jnp.moveaxis(chunk_states, 1, 0),
                                    jnp.moveaxis(chunk_decay, 1, 0)))
    prev = jnp.moveaxis(prev, 0, 1)
    y_off = jnp.einsum("bclgn,bcgrpn,bcgrl->bclgrp", Cc, prev, jnp.exp(a_cs))
    return (y_diag + y_off).reshape(b, l, h, p)


def mamba2_mixer(h, w_in, conv_w, conv_b, dt_bias, a_log, d_skip, norm_w, w_out):
    b, l, _ = h.shape
    f32 = jnp.float32
    z, xbc, dt = jnp.split(h @ w_in, [D_INNER, D_INNER + CONV_DIM], axis=-1)
    xbc = jax.nn.silu(causal_dwconv(xbc, conv_w, conv_b))
    xs, Bm, Cm = jnp.split(xbc, [D_INNER, D_INNER + SSM_GROUPS * SSM_STATE], axis=-1)
    dt = jax.nn.softplus(dt.astype(f32) + dt_bias.astype(f32))
    A = -jnp.exp(a_log.astype(f32))
    xs_h = xs.astype(f32).reshape(b, l, SSM_HEADS, SSM_HEADDIM)
    y = ssd_scan(xs_h, dt, A,
                 Bm.astype(f32).reshape(b, l, SSM_GROUPS, SSM_STATE),
                 Cm.astype(f32).reshape(b, l, SSM_GROUPS, SSM_STATE))
    y = y + d_skip.astype(f32)[:, None] * xs_h
    y = y.reshape(b, l, D_INNER) * jax.nn.silu(z.astype(f32))
    yg = y.reshape(b, l, SSM_GROUPS, D_INNER // SSM_GROUPS)
    yg = yg * lax.rsqrt(jnp.mean(yg * yg, axis=-1, keepdims=True) + EPS)
    y = yg.reshape(b, l, D_INNER) * norm_w.astype(f32)
    return y.astype(h.dtype) @ w_out


def to_bands(t):
    b, l = t.shape[:2]
    nb = l // WINDOW
    blocks = t.reshape(b, nb, WINDOW, *t.shape[2:])
    prev = jnp.pad(blocks[:, :-1], ((0, 0), (1, 0)) + ((0, 0),) * (blocks.ndim - 2))
    return jnp.concatenate([prev, blocks], axis=2)


def shared_kv(x, norm_w, shift, scale, w_kv, b_kv):
    b, l, _ = x.shape
    h = modulate(rmsnorm(x, norm_w), shift, scale)
    k, v = jnp.split(h @ w_kv + b_kv, 2, axis=-1)
    k = k.reshape(b, l, KV_HEADS, HEAD_DIM)
    v = v.reshape(b, l, KV_HEADS, HEAD_DIM)
    return to_bands(k), to_bands(v)


def swa_sink_attention(h, k_band, v_band, w_q, b_q, sinks, w_o, b_o):
    b, l, _ = h.shape
    nb = l // WINDOW
    r = ATT_HEADS // KV_HEADS
    f32 = jnp.float32
    q = (h @ w_q + b_q).reshape(b, nb, WINDOW, KV_HEADS, r, HEAD_DIM)
    s = jnp.einsum("bnqkrd,bnskd->bkrnqs", q.astype(f32), k_band.astype(f32)) * HEAD_DIM ** -0.5
    qpos = jnp.arange(WINDOW)[:, None] + WINDOW
    kpos = jnp.arange(2 * WINDOW)[None, :]
    local = (kpos <= qpos) & (kpos > qpos - WINDOW)
    mask = local[None] & ((jnp.arange(nb)[:, None, None] > 0) | (kpos[None] >= WINDOW))
    logits = jnp.where(mask, s, -jnp.inf)
    sink = sinks.astype(f32).reshape(1, KV_HEADS, r, 1, 1)
    m = jnp.maximum(jnp.max(logits, axis=-1), sink)
    p = jnp.exp(logits - m[..., None])
    denom = jnp.sum(p, axis=-1) + jnp.exp(sink - m)
    probs = (p / denom[..., None]).astype(v_band.dtype)
    o = jnp.einsum("bkrnqs,bnskd->bnqkrd", probs, v_band)
    return o.reshape(b, l, ATT_HEADS * HEAD_DIM) @ w_o + b_o


def setup_inputs(seed: int = 0) -> dict:
    key = jax.random.key(seed)
    ks = jax.random.split(key, 32)
    f32 = jnp.float32
    nrm = lambda k, shape, s: jax.random.normal(k, shape, f32) * s
    D = D_MODEL
    dt0 = jnp.exp(jax.random.uniform(ks[10], (N_A_LAYERS, SSM_HEADS), f32,
                                     np.log(1e-3), np.log(1e-1)))
    dt_bias = dt0 + jnp.log(-jnp.expm1(-dt0))
    return {
        "x": nrm(ks[0], (BATCH, SEQ, D), 1.0),
        "c": nrm(ks[1], (BATCH, D), 1.0),
        "ffn_norm_w": 1.0 + nrm(ks[2], (DEPTH, 2, D), 0.02),
        "ffn_w_gu": nrm(ks[3], (DEPTH, 2, D, 2 * D_FF), D ** -0.5),
        "ffn_w_down": nrm(ks[4], (DEPTH, 2, D_FF, D), D_FF ** -0.5),
        "mod_w": nrm(ks[5], (DEPTH, D, N_MOD * D), 0.5 * D ** -0.5),
        "mod_b": nrm(ks[6], (DEPTH, N_MOD * D), 0.02),
        "mix_norm_w": 1.0 + nrm(ks[7], (DEPTH, D), 0.02),
        "ssm_w_in": nrm(ks[8], (N_A_LAYERS, D, IN_PROJ_DIM), D ** -0.5),
        "ssm_conv_w": nrm(ks[9], (N_A_LAYERS, CONV_WIDTH, CONV_DIM), CONV_WIDTH ** -0.5),
        "ssm_conv_b": nrm(ks[11], (N_A_LAYERS, CONV_DIM), 0.02),
        "ssm_dt_bias": dt_bias,
        "ssm_a_log": jnp.log(jax.random.uniform(ks[12], (N_A_LAYERS, SSM_HEADS), f32, 1.0, 16.0)),
        "ssm_d": 1.0 + nrm(ks[13], (N_A_LAYERS, SSM_HEADS), 0.1),
        "ssm_norm_w": 1.0 + nrm(ks[14], (N_A_LAYERS, D_INNER), 0.02),
        "ssm_w_out": nrm(ks[15], (N_A_LAYERS, D_INNER, D), D_INNER ** -0.5),
        "kv_norm_w": 1.0 + nrm(ks[16], (D,), 0.02),
        "kv_mod_w": nrm(ks[17], (D, 2 * D), 0.5 * D ** -0.5),
        "kv_mod_b": nrm(ks[18], (2 * D,), 0.02),
        "w_kv": nrm(ks[19], (D, 2 * KV_HEADS * HEAD_DIM), D ** -0.5),
        "b_kv": nrm(ks[20], (2 * KV_HEADS * HEAD_DIM,), 0.02),
        "attn_w_q": nrm(ks[21], (N_B_LAYERS, D, ATT_HEADS * HEAD_DIM), D ** -0.5),
        "attn_b_q": nrm(ks[22], (N_B_LAYERS, ATT_HEADS * HEAD_DIM), 0.02),
        "attn_sinks": nrm(ks[23], (N_B_LAYERS, ATT_HEADS), 0.5),
        "attn_w_o": nrm(ks[24], (N_B_LAYERS, ATT_HEADS * HEAD_DIM, D), (ATT_HEADS * HEAD_DIM) ** -0.5),
        "attn_b_o": nrm(ks[25], (N_B_LAYERS, D), 0.02),
        "final_norm_w": 1.0 + nrm(ks[26], (D,), 0.02),
    }


def reference(x, c, ffn_norm_w, ffn_w_gu, ffn_w_down, mod_w, mod_b, mix_norm_w,
              ssm_w_in, ssm_conv_w, ssm_conv_b, ssm_dt_bias, ssm_a_log, ssm_d,
              ssm_norm_w, ssm_w_out, kv_norm_w, kv_mod_w, kv_mod_b, w_kv, b_kv,
              attn_w_q, attn_b_q, attn_sinks, attn_w_o, attn_b_o, final_norm_w):
    c_act = jax.nn.silu(c)
    k_band = v_band = None
    for i in range(DEPTH):
        if i == N_A_LAYERS:
            kv_shift, kv_scale = jnp.split(c_act @ kv_mod_w + kv_mod_b, 2, axis=-1)
            k_band, v_band = shared_kv(x, kv_norm_w, kv_shift, kv_scale, w_kv, b_kv)
        mod = c_act @ mod_w[i] + mod_b[i]
        sh1, sc1, g1, shm, scm, gm, sh2, sc2, g2 = jnp.split(mod, N_MOD, axis=-1)
        h = modulate(rmsnorm(x, ffn_norm_w[i, 0]), sh1, sc1)
        x = x + FFN_HALF * g1[:, None, :] * swiglu(h, ffn_w_gu[i, 0], ffn_w_down[i, 0])
        h = modulate(rmsnorm(x, mix_norm_w[i]), shm, scm)
        if i < N_A_LAYERS:
            j = i
            y = mamba2_mixer(h, ssm_w_in[j], ssm_conv_w[j], ssm_conv_b[j], ssm_dt_bias[j],
                             ssm_a_log[j], ssm_d[j], ssm_norm_w[j], ssm_w_out[j])
        else:
            j = i - N_A_LAYERS
            y = swa_sink_attention(h, k_band, v_band, attn_w_q[j], attn_b_q[j],
                                   attn_sinks[j], attn_w_o[j], attn_b_o[j])
        x = x + gm[:, None, :] * y
        h = modulate(rmsnorm(x, ffn_norm_w[i, 1]), sh2, sc2)
        x = x + FFN_HALF * g2[:, None, :] * swiglu(h, ffn_w_gu[i, 1], ffn_w_down[i, 1])
    return rmsnorm(x, final_norm_w)
```

```python
import functools

import jax
import jax.numpy as jnp
from jax import lax
from jax.experimental import pallas as pl
from jax.experimental.pallas import tpu as pltpu

F32 = jnp.float32
BF16 = jnp.bfloat16

D_MODEL = 1024
BATCH = 8
SEQ = 2048
TOKENS = BATCH * SEQ
DEPTH = 4
N_A_LAYERS = DEPTH // 2
EPS = 1e-5
N_MOD = 9
D_FF = 2816
FFN_HALF = 0.5
D_INNER = 2048
SSM_HEADDIM = 64
SSM_HEADS = 32
SSM_GROUPS = 8
SSM_STATE = 128
HEADS_PER_GROUP = SSM_HEADS // SSM_GROUPS
GROUP_WIDTH = D_INNER // SSM_GROUPS
CONV_WIDTH = 4
CHUNK = 128
CONV_DIM = D_INNER + 2 * SSM_GROUPS * SSM_STATE
ATT_HEADS = 16
KV_HEADS = 4
Q_PER_KV = ATT_HEADS // KV_HEADS
HEAD_DIM = 64
WINDOW = 128
KV_DIM = KV_HEADS * HEAD_DIM

LANES = 128
CONV_CARRY = 8
VMEM_LIMIT = 60 * 1024 * 1024

TM_FFN = 512
TM_PROJ = 512
TL_ATT = 512
FF_CHUNKS = ((0, 1024), (1024, 2048), (2048, D_FF))
CONV_LANE_CHUNK = 512
NEG_BIG = -1e30


def _silu(v):
    return v / (1.0 + jnp.exp(-v))


def _rms_mod(x, nw, shift, scale):
    ms = jnp.mean(x * x, axis=-1, keepdims=True)
    y = x * lax.rsqrt(ms + EPS) * nw
    return y * (1.0 + scale) + shift


def _bdot(a, b):
    return jnp.dot(a, b, preferred_element_type=F32)


def _params(sem):
    return pltpu.CompilerParams(dimension_semantics=sem, vmem_limit_bytes=VMEM_LIMIT)


def _resident(shape, index_map):
    return pl.BlockSpec(shape, index_map, pipeline_mode=pl.Buffered(1))


def _mod_kernel(c_ref, w_ref, b_ref, o_ref):
    ca = _silu(c_ref[...]).astype(BF16)
    o_ref[0] = _bdot(ca, w_ref[0].astype(BF16)) + b_ref[0]


def _mod_call(c, w, b, n_tile):
    nl, _, n = w.shape
    return pl.pallas_call(
        _mod_kernel,
        grid=(nl, n // n_tile),
        in_specs=[
            pl.BlockSpec((BATCH, D_MODEL), lambda l, j: (0, 0)),
            pl.BlockSpec((1, D_MODEL, n_tile), lambda l, j: (l, 0, j)),
            pl.BlockSpec((1, 1, n_tile), lambda l, j: (l, 0, j)),
        ],
        out_specs=pl.BlockSpec((1, BATCH, n_tile), lambda l, j: (l, 0, j)),
        out_shape=jax.ShapeDtypeStruct((nl, BATCH, n), F32),
        compiler_params=_params(("arbitrary", "arbitrary")),
        name="adaln_mod",
    )(c, w, b)


def _ffn_kernel(x_ref, mod_ref, nw_ref, wgu_ref, wd_ref, fnw_ref, o_ref, *, k0, final):
    x = x_ref[...]
    shift = mod_ref[0, k0:k0 + 1, :]
    scale = mod_ref[0, k0 + 1:k0 + 2, :]
    gate = mod_ref[0, k0 + 2:k0 + 3, :]
    h = _rms_mod(x, nw_ref[...], shift, scale).astype(BF16)
    acc = None
    for c0, c1 in FF_CHUNKS:
        g = _bdot(h, wgu_ref[:, c0:c1])
        u = _bdot(h, wgu_ref[:, D_FF + c0:D_FF + c1])
        a = (_silu(g) * u).astype(BF16)
        part = _bdot(a, wd_ref[c0:c1, :])
        acc = part if acc is None else acc + part
    out = x + (FFN_HALF * gate) * acc
    if final:
        ms = jnp.mean(out * out, axis=-1, keepdims=True)
        out = out * lax.rsqrt(ms + EPS) * fnw_ref[...]
    o_ref[...] = out


def _ffn_call(x, mod, norm_w, w_gu, w_down, layer, half, final_w, final):
    per_seq = SEQ // TM_FFN
    k0 = 0 if half == 0 else 6
    return pl.pallas_call(
        functools.partial(_ffn_kernel, k0=k0, final=final),
        grid=(TOKENS // TM_FFN,),
        in_specs=[
            pl.BlockSpec((TM_FFN, D_MODEL), lambda i: (i, 0)),
            pl.BlockSpec((1, N_MOD, D_MODEL), lambda i: (i // per_seq, 0, 0)),
            pl.BlockSpec((None, None, 1, D_MODEL), lambda i: (layer, half, 0, 0)),
            _resident((None, None, D_MODEL, 2 * D_FF), lambda i: (layer, half, 0, 0)),
            _resident((None, None, D_FF, D_MODEL), lambda i: (layer, half, 0, 0)),
            pl.BlockSpec((1, D_MODEL), lambda i: (0, 0)),
        ],
        out_specs=pl.BlockSpec((TM_FFN, D_MODEL), lambda i: (i, 0)),
        out_shape=jax.ShapeDtypeStruct((TOKENS, D_MODEL), F32),
        compiler_params=_params(("arbitrary",)),
        name="swiglu_half",
    )(x, mod, norm_w, w_gu, w_down, final_w)


def _ssm_in_kernel(x_ref, mod_ref, nw_ref, wzx_ref, wdt_ref, cw_ref, cb_ref, dtb_ref,
                   z_ref, xbc_ref, dt_ref, buf_ref):
    j = pl.program_id(1)
    tl = x_ref.shape[0]
    x = x_ref[...]
    h = _rms_mod(x, nw_ref[...], mod_ref[0, 3:4, :], mod_ref[0, 4:5, :]).astype(BF16)
    z_ref[...] = _bdot(h, wzx_ref[:, :D_INNER])

    @pl.when(j == 0)
    def _():
        buf_ref[0:CONV_CARRY, :] = jnp.zeros((CONV_CARRY, CONV_DIM), F32)

    buf_ref[CONV_CARRY:CONV_CARRY + tl, :] = _bdot(h, wzx_ref[:, D_INNER:])

    def lane_chunk(ci, carry):
        lanes = pl.ds(pl.multiple_of(ci * CONV_LANE_CHUNK, CONV_LANE_CHUNK), CONV_LANE_CHUNK)
        acc = cb_ref[:, lanes]
        for k in range(CONV_WIDTH):
            r0 = CONV_CARRY - (CONV_WIDTH - 1) + k
            acc = acc + cw_ref[k:k + 1, lanes] * buf_ref[r0:r0 + tl, lanes]
        xbc_ref[:, lanes] = _silu(acc)
        return carry

    lax.fori_loop(0, CONV_DIM // CONV_LANE_CHUNK, lane_chunk, 0)
    buf_ref[0:CONV_CARRY, :] = buf_ref[tl:tl + CONV_CARRY, :]

    dt_raw = _bdot(h, wdt_ref[...]) + dtb_ref[...]
    dt_ref[...] = jnp.maximum(dt_raw, 0.0) + jnp.log(1.0 + jnp.exp(-jnp.abs(dt_raw)))


def _ssm_in_call(x, mod, norm_w, w_zx, w_dt, conv_w, conv_b, dt_bias, layer):
    per_seq = SEQ // TM_PROJ
    row = lambda b, j: (b * per_seq + j, 0)
    return pl.pallas_call(
        _ssm_in_kernel,
        grid=(BATCH, per_seq),
        in_specs=[
            pl.BlockSpec((TM_PROJ, D_MODEL), row),
            pl.BlockSpec((1, N_MOD, D_MODEL), lambda b, j: (b, 0, 0)),
            pl.BlockSpec((None, 1, D_MODEL), lambda b, j: (layer, 0, 0)),
            _resident((None, D_MODEL, D_INNER + CONV_DIM), lambda b, j: (layer, 0, 0)),
            _resident((None, D_MODEL, LANES), lambda b, j: (layer, 0, 0)),
            pl.BlockSpec((None, CONV_WIDTH, CONV_DIM), lambda b, j: (layer, 0, 0)),
            pl.BlockSpec((None, 1, CONV_DIM), lambda b, j: (layer, 0, 0)),
            pl.BlockSpec((None, 1, LANES), lambda b, j: (layer, 0, 0)),
        ],
        out_specs=[
            pl.BlockSpec((TM_PROJ, D_INNER), row),
            pl.BlockSpec((TM_PROJ, CONV_DIM), row),
            pl.BlockSpec((TM_PROJ, LANES), row),
        ],
        out_shape=[
            jax.ShapeDtypeStruct((TOKENS, D_INNER), F32),
            jax.ShapeDtypeStruct((TOKENS, CONV_DIM), F32),
            jax.ShapeDtypeStruct((TOKENS, LANES), F32),
        ],
        scratch_shapes=[pltpu.VMEM((TM_PROJ + CONV_CARRY, CONV_DIM), F32)],
        compiler_params=_params(("arbitrary", "arbitrary")),
        name="ssm_in_conv",
    )(x, mod, norm_w, w_zx, w_dt, conv_w, conv_b, dt_bias)


def _split2(v):
    hi = v.astype(BF16)
    lo = (v - hi.astype(F32)).astype(BF16)
    return jnp.concatenate([hi, lo], axis=1)


def _ssd_kernel(xs_ref, b_ref, c_ref, dt_ref, alog_ref, dx_ref, exp_ref, y_ref, state_ref):
    j = pl.program_id(1)

    @pl.when(j == 0)
    def _():
        state_ref[...] = jnp.zeros(state_ref.shape, F32)

    head_lane = lax.broadcasted_iota(jnp.int32, (1, LANES), 1) < SSM_HEADS
    a_neg = jnp.where(head_lane, -jnp.exp(alog_ref[...]), 0.0)
    dt = dt_ref[...]
    a = dt * a_neg

    a_hi = a.astype(BF16)
    r1 = a - a_hi.astype(F32)
    a_mid = r1.astype(BF16)
    a_lo = (r1 - a_mid.astype(F32)).astype(BF16)
    row = lax.broadcasted_iota(jnp.int32, (CHUNK, CHUNK), 0)
    col = lax.broadcasted_iota(jnp.int32, (CHUNK, CHUNK), 1)
    causal = row >= col
    tri = jnp.where(causal, 1.0, 0.0).astype(BF16)
    cs3 = _bdot(tri, jnp.concatenate([a_hi, a_mid, a_lo], axis=1))
    acs = cs3[:, :LANES] + cs3[:, LANES:2 * LANES] + cs3[:, 2 * LANES:]
    acs_t = acs.T
    total = acs[CHUNK - 1:CHUNK, :]

    expand = exp_ref[...]
    dt_x = _bdot(_split2(dt), expand)
    decay_end_x = _bdot(_split2(jnp.exp(total - acs)), expand)
    decay_in_x = _bdot(_split2(jnp.exp(acs)), expand)
    chunk_decay_x = _bdot(_split2(jnp.broadcast_to(jnp.exp(total), (8, LANES))), expand)[0:1, :]

    xs = xs_ref[...]
    x_dt = xs * dt_x
    x_b = x_dt.astype(BF16)
    x_end = (x_dt * decay_end_x).astype(BF16)
    head_of_lane = lax.broadcasted_iota(jnp.int32, (CHUNK, GROUP_WIDTH), 1) // SSM_HEADDIM

    for g in range(SSM_GROUPS):
        n0 = g * SSM_STATE
        w0 = g * GROUP_WIDTH
        bg = b_ref[:, n0:n0 + SSM_STATE]
        cg = c_ref[:, n0:n0 + SSM_STATE].astype(BF16)
        cb = lax.dot_general(cg, bg.astype(BF16), (((1,), (1,)), ((), ())),
                             preferred_element_type=F32)
        xg = x_b[:, w0:w0 + GROUP_WIDTH]
        m_parts, x_parts = [], []
        for r in range(HEADS_PER_GROUP):
            hh = g * HEADS_PER_GROUP + r
            seg = acs[:, hh:hh + 1] - acs_t[hh:hh + 1, :]
            decay = jnp.exp(jnp.where(causal, seg, NEG_BIG))
            m_parts.append((cb * decay).astype(BF16))
            x_parts.append(jnp.where(head_of_lane == r, xg, jnp.zeros_like(xg)))
        y_diag = _bdot(jnp.concatenate(m_parts, axis=1), jnp.concatenate(x_parts, axis=0))

        s_prev = state_ref[g]
        y_off = _bdot(cg, s_prev.astype(BF16)) * decay_in_x[:, w0:w0 + GROUP_WIDTH]
        y_ref[:, w0:w0 + GROUP_WIDTH] = (y_diag + y_off
                                         + dx_ref[:, w0:w0 + GROUP_WIDTH] * xs[:, w0:w0 + GROUP_WIDTH])
        s_chunk = _bdot(bg.T.astype(BF16), x_end[:, w0:w0 + GROUP_WIDTH])
        state_ref[g] = s_prev * chunk_decay_x[:, w0:w0 + GROUP_WIDTH] + s_chunk


def _ssd_call(xbc, dt, a_log, d_x, expand, layer):
    per_seq = SEQ // CHUNK
    row = lambda b, j: (b * per_seq + j, 0)
    b_blk = D_INNER // (SSM_GROUPS * SSM_STATE)
    return pl.pallas_call(
        _ssd_kernel,
        grid=(BATCH, per_seq),
        in_specs=[
            pl.BlockSpec((CHUNK, D_INNER), row),
            pl.BlockSpec((CHUNK, SSM_GROUPS * SSM_STATE), lambda b, j: (b * per_seq + j, b_blk)),
            pl.BlockSpec((CHUNK, SSM_GROUPS * SSM_STATE), lambda b, j: (b * per_seq + j, b_blk + 1)),
            pl.BlockSpec((CHUNK, LANES), row),
            pl.BlockSpec((None, 1, LANES), lambda b, j: (layer, 0, 0)),
            pl.BlockSpec((None, 1, D_INNER), lambda b, j: (layer, 0, 0)),
            pl.BlockSpec((2 * LANES, D_INNER), lambda b, j: (0, 0)),
        ],
        out_specs=pl.BlockSpec((CHUNK, D_INNER), row),
        out_shape=jax.ShapeDtypeStruct((TOKENS, D_INNER), F32),
        scratch_shapes=[pltpu.VMEM((SSM_GROUPS, SSM_STATE, GROUP_WIDTH), F32)],
        compiler_params=_params(("arbitrary", "arbitrary")),
        name="ssd_chunk_scan",
    )(xbc, xbc, xbc, dt, a_log, d_x, expand)


def _ssm_out_kernel(y_ref, z_ref, x_ref, mod_ref, nw_ref, wo_ref, o_ref):
    y = y_ref[...] * _silu(z_ref[...])
    parts = []
    for g in range(SSM_GROUPS):
        blk = y[:, g * GROUP_WIDTH:(g + 1) * GROUP_WIDTH]
        ms = jnp.mean(blk * blk, axis=-1, keepdims=True)
        parts.append(blk * lax.rsqrt(ms + EPS))
    yn = (jnp.concatenate(parts, axis=1) * nw_ref[...]).astype(BF16)
    o_ref[...] = x_ref[...] + mod_ref[0, 5:6, :] * _bdot(yn, wo_ref[...])


def _ssm_out_call(y, z, x, mod, norm_w, w_out, layer):
    per_seq = SEQ // TM_PROJ
    return pl.pallas_call(
        _ssm_out_kernel,
        grid=(TOKENS // TM_PROJ,),
        in_specs=[
            pl.BlockSpec((TM_PROJ, D_INNER), lambda i: (i, 0)),
            pl.BlockSpec((TM_PROJ, D_INNER), lambda i: (i, 0)),
            pl.BlockSpec((TM_PROJ, D_MODEL), lambda i: (i, 0)),
            pl.BlockSpec((1, N_MOD, D_MODEL), lambda i: (i // per_seq, 0, 0)),
            pl.BlockSpec((None, 1, D_INNER), lambda i: (layer, 0, 0)),
            _resident((None, D_INNER, D_MODEL), lambda i: (layer, 0, 0)),
        ],
        out_specs=pl.BlockSpec((TM_PROJ, D_MODEL), lambda i: (i, 0)),
        out_shape=jax.ShapeDtypeStruct((TOKENS, D_MODEL), F32),
        compiler_params=_params(("arbitrary",)),
        name="ssm_gate_norm_out",
    )(y, z, x, mod, norm_w, w_out)


def _kv_kernel(x_ref, mod_ref, nw_ref, w_ref, b_ref, k_ref, v_ref):
    h = _rms_mod(x_ref[...], nw_ref[...], mod_ref[0, 0:1, :], mod_ref[0, 1:2, :]).astype(BF16)
    kv = _bdot(h, w_ref[...]) + b_ref[...]
    k_ref[...] = kv[:, :KV_DIM].astype(BF16)
    v_ref[...] = kv[:, KV_DIM:].astype(BF16)


def _kv_call(x, kv_mod, norm_w, w_kv, b_kv):
    per_seq = SEQ // TM_PROJ
    return pl.pallas_call(
        _kv_kernel,
        grid=(TOKENS // TM_PROJ,),
        in_specs=[
            pl.BlockSpec((TM_PROJ, D_MODEL), lambda i: (i, 0)),
            pl.BlockSpec((1, 2, D_MODEL), lambda i: (i // per_seq, 0, 0)),
            pl.BlockSpec((1, D_MODEL), lambda i: (0, 0)),
            _resident((D_MODEL, 2 * KV_DIM), lambda i: (0, 0)),
            pl.BlockSpec((1, 2 * KV_DIM), lambda i: (0, 0)),
        ],
        out_specs=[
            pl.BlockSpec((TM_PROJ, KV_DIM), lambda i: (i, 0)),
            pl.BlockSpec((TM_PROJ, KV_DIM), lambda i: (i, 0)),
        ],
        out_shape=[
            jax.ShapeDtypeStruct((TOKENS, KV_DIM), BF16),
            jax.ShapeDtypeStruct((TOKENS, KV_DIM), BF16),
        ],
        compiler_params=_params(("arbitrary",)),
        name="shared_kv",
    )(x, kv_mod, norm_w, w_kv, b_kv)


def _attn_kernel(sink_ref, x_ref, mod_ref, nw_ref, wq_ref, bq_ref, kc_ref, kp_ref, vc_ref, vp_ref,
                 wo_ref, bo_ref, o_ref, *, layer):
    j = pl.program_id(1)
    tl = x_ref.shape[0]
    x = x_ref[...]
    h = _rms_mod(x, nw_ref[...], mod_ref[0, 3:4, :], mod_ref[0, 4:5, :]).astype(BF16)
    q = ((_bdot(h, wq_ref[...]) + bq_ref[...]) * (HEAD_DIM ** -0.5)).astype(BF16)
    k_all = jnp.concatenate([kp_ref[...], kc_ref[...]], axis=0)
    v_all = jnp.concatenate([vp_ref[...], vc_ref[...]], axis=0)

    qpos = lax.broadcasted_iota(jnp.int32, (WINDOW, 2 * WINDOW), 0)
    kpos = lax.broadcasted_iota(jnp.int32, (WINDOW, 2 * WINDOW), 1)
    local = (kpos > qpos) & (kpos <= qpos + WINDOW)
    first_key = jnp.where(j == 0, WINDOW, 0)
    out_blocks = []
    for n in range(tl // WINDOW):
        mask1 = (local & (kpos >= first_key)) if n == 0 else local
        mask = jnp.concatenate([mask1] * Q_PER_KV, axis=0)
        rows = slice(n * WINDOW, (n + 1) * WINDOW)
        band = slice(n * WINDOW, (n + 2) * WINDOW)
        head_out = []
        for kh in range(KV_HEADS):
            cols = slice(kh * HEAD_DIM, (kh + 1) * HEAD_DIM)
            q4 = jnp.concatenate(
                [q[rows, (kh * Q_PER_KV + r) * HEAD_DIM:(kh * Q_PER_KV + r + 1) * HEAD_DIM]
                 for r in range(Q_PER_KV)], axis=0)
            s = lax.dot_general(q4, k_all[band, cols], (((1,), (1,)), ((), ())),
                                preferred_element_type=F32)
            s = jnp.where(mask, s, NEG_BIG)
            sink = jnp.concatenate(
                [jnp.full((WINDOW, 1), sink_ref[layer, kh * Q_PER_KV + r], F32) for r in range(Q_PER_KV)],
                axis=0)
            m = jnp.maximum(jnp.max(s, axis=-1, keepdims=True), sink)
            p = jnp.exp(s - m)
            denom = jnp.sum(p, axis=-1, keepdims=True) + jnp.exp(sink - m)
            probs = (p / denom).astype(BF16)
            o4 = _bdot(probs, v_all[band, cols])
            head_out += [o4[r * WINDOW:(r + 1) * WINDOW, :] for r in range(Q_PER_KV)]
        out_blocks.append(jnp.concatenate(head_out, axis=1))
    o = jnp.concatenate(out_blocks, axis=0).astype(BF16)
    o_ref[...] = x + mod_ref[0, 5:6, :] * (_bdot(o, wo_ref[...]) + bo_ref[...])


def _attn_call(sinks, x, mod, norm_w, w_q, b_q, k, v, w_o, b_o, layer, att_layer):
    per_seq = SEQ // TL_ATT
    blocks_per_tile = TL_ATT // WINDOW
    row = lambda b, j: (b * per_seq + j, 0)
    prev = lambda b, j: (jnp.maximum((b * per_seq + j) * blocks_per_tile - 1, 0), 0)
    return pl.pallas_call(
        functools.partial(_attn_kernel, layer=att_layer),
        grid=(BATCH, per_seq),
        in_specs=[
            pl.BlockSpec(memory_space=pltpu.SMEM),
            pl.BlockSpec((TL_ATT, D_MODEL), row),
            pl.BlockSpec((1, N_MOD, D_MODEL), lambda b, j: (b, 0, 0)),
            pl.BlockSpec((None, 1, D_MODEL), lambda b, j: (layer, 0, 0)),
            _resident((None, D_MODEL, ATT_HEADS * HEAD_DIM), lambda b, j: (att_layer, 0, 0)),
            pl.BlockSpec((None, 1, ATT_HEADS * HEAD_DIM), lambda b, j: (att_layer, 0, 0)),
            pl.BlockSpec((TL_ATT, KV_DIM), row),
            pl.BlockSpec((WINDOW, KV_DIM), prev),
            pl.BlockSpec((TL_ATT, KV_DIM), row),
            pl.BlockSpec((WINDOW, KV_DIM), prev),
            _resident((None, ATT_HEADS * HEAD_DIM, D_MODEL), lambda b, j: (att_layer, 0, 0)),
            pl.BlockSpec((None, 1, D_MODEL), lambda b, j: (att_layer, 0, 0)),
        ],
        out_specs=pl.BlockSpec((TL_ATT, D_MODEL), row),
        out_shape=jax.ShapeDtypeStruct((TOKENS, D_MODEL), F32),
        compiler_params=_params(("arbitrary", "arbitrary")),
        name="swa_sink_attention",
    )(sinks, x, mod, norm_w, w_q, b_q, k, k, v, v, w_o, b_o)


def _head_expand_matrix():
    r = lax.broadcasted_iota(jnp.int32, (2 * LANES, D_INNER), 0) % LANES
    ch = lax.broadcasted_iota(jnp.int32, (2 * LANES, D_INNER), 1) // SSM_HEADDIM
    return (r == ch).astype(BF16)


def kernel(x, c, ffn_norm_w, ffn_w_gu, ffn_w_down, mod_w, mod_b, mix_norm_w, ssm_w_in, ssm_conv_w,
           ssm_conv_b, ssm_dt_bias, ssm_a_log, ssm_d, ssm_norm_w, ssm_w_out, kv_norm_w, kv_mod_w,
           kv_mod_b, w_kv, b_kv, attn_w_q, attn_b_q, attn_sinks, attn_w_o, attn_b_o, final_norm_w):
    w_gu = ffn_w_gu.astype(BF16)
    w_down = ffn_w_down.astype(BF16)
    w_zx = ssm_w_in[:, :, :D_INNER + CONV_DIM].astype(BF16)
    pad = LANES - SSM_HEADS
    w_dt = jnp.pad(ssm_w_in[:, :, D_INNER + CONV_DIM:], ((0, 0), (0, 0), (0, pad))).astype(BF16)
    dt_bias = jnp.pad(ssm_dt_bias, ((0, 0), (0, pad)))[:, None, :]
    a_log = jnp.pad(ssm_a_log, ((0, 0), (0, pad)))[:, None, :]
    d_x = jnp.repeat(ssm_d, SSM_HEADDIM, axis=1)[:, None, :]
    w_out = ssm_w_out.astype(BF16)
    w_kv_b = w_kv.astype(BF16)
    w_q = attn_w_q.astype(BF16)
    w_o = attn_w_o.astype(BF16)
    expand = _head_expand_matrix()

    mod_all = _mod_call(c, mod_w, mod_b[:, None, :], (N_MOD * D_MODEL) // 4)
    kv_mod = _mod_call(c, kv_mod_w[None], kv_mod_b[None, None, :], 2 * D_MODEL)

    xf = x.reshape(TOKENS, D_MODEL)
    final_w = final_norm_w[None, :]
    k = v = None
    for i in range(DEPTH):
        mod = mod_all[i].reshape(BATCH, N_MOD, D_MODEL)
        if i == N_A_LAYERS:
            k, v = _kv_call(xf, kv_mod[0].reshape(BATCH, 2, D_MODEL), kv_norm_w[None, :],
                            w_kv_b, b_kv[None, :])
        xf = _ffn_call(xf, mod, ffn_norm_w[:, :, None, :], w_gu, w_down, i, 0, final_w, False)
        if i < N_A_LAYERS:
            z, xbc, dt = _ssm_in_call(xf, mod, mix_norm_w[:, None, :], w_zx, w_dt, ssm_conv_w,
                                      ssm_conv_b[:, None, :], dt_bias, i)
            y = _ssd_call(xbc, dt, a_log, d_x, expand, i)
            xf = _ssm_out_call(y, z, xf, mod, ssm_norm_w[:, None, :], w_out, i)
        else:
            jj = i - N_A_LAYERS
            xf = _attn_call(attn_sinks, xf, mod, mix_norm_w[:, None, :], w_q, attn_b_q[:, None, :],
                            k, v, w_o, attn_b_o[:, None, :], i, jj)
        xf = _ffn_call(xf, mod, ffn_norm_w[:, :, None, :], w_gu, w_down, i, 1, final_w,
                       i == DEPTH - 1)
    return xf.reshape(BATCH, SEQ, D_MODEL)
```

```python
import functools

import jax
import jax.numpy as jnp
from jax import lax
from jax.experimental import pallas as pl
from jax.experimental.pallas import tpu as pltpu

F32 = jnp.float32
BF16 = jnp.bfloat16

D_MODEL = 1024
BATCH = 8
SEQ = 2048
TOKENS = BATCH * SEQ
DEPTH = 4
N_A_LAYERS = DEPTH // 2
EPS = 1e-5
N_MOD = 9
D_FF = 2816
FFN_HALF = 0.5
D_INNER = 2048
SSM_HEADDIM = 64
SSM_HEADS = 32
SSM_GROUPS = 8
SSM_STATE = 128
HEADS_PER_GROUP = SSM_HEADS // SSM_GROUPS
GROUP_WIDTH = D_INNER // SSM_GROUPS
CONV_WIDTH = 4
CHUNK = 128
CONV_DIM = D_INNER + 2 * SSM_GROUPS * SSM_STATE
ATT_HEADS = 16
KV_HEADS = 4
Q_PER_KV = ATT_HEADS // KV_HEADS
HEAD_DIM = 64
WINDOW = 128
KV_DIM = KV_HEADS * HEAD_DIM

LANES = 128
CONV_CARRY = 8
VMEM_LIMIT = 60 * 1024 * 1024

TM_FFN = 512
TM_PROJ = 512
TL_ATT = 512
FF_CHUNKS = ((0, 1024), (1024, 2048), (2048, D_FF))
CONV_LANE_CHUNK = 512
NEG_BIG = -1e30


def _silu(v):
    return v / (1.0 + jnp.exp(-v))


def _rms_mod(x, nw, shift, scale):
    ms = jnp.mean(x * x, axis=-1, keepdims=True)
    y = x * lax.rsqrt(ms + EPS) * nw
    return y * (1.0 + scale) + shift


def _bdot(a, b):
    return jnp.dot(a, b, preferred_element_type=F32)


def _params(sem):
    return pltpu.CompilerParams(dimension_semantics=sem, vmem_limit_bytes=VMEM_LIMIT)


def _resident(shape, index_map):
    return pl.BlockSpec(shape, index_map, pipeline_mode=pl.Buffered(1))


def _mod_kernel(c_ref, w_ref, b_ref, o_ref):
    ca = _silu(c_ref[...]).astype(BF16)
    o_ref[0] = _bdot(ca, w_ref[0].astype(BF16)) + b_ref[0]


def _mod_call(c, w, b, n_tile):
    nl, _, n = w.shape
    return pl.pallas_call(
        _mod_kernel,
        grid=(nl, n // n_tile),
        in_specs=[
            pl.BlockSpec((BATCH, D_MODEL), lambda l, j: (0, 0)),
            pl.BlockSpec((1, D_MODEL, n_tile), lambda l, j: (l, 0, j)),
            pl.BlockSpec((1, 1, n_tile), lambda l, j: (l, 0, j)),
        ],
        out_specs=pl.BlockSpec((1, BATCH, n_tile), lambda l, j: (l, 0, j)),
        out_shape=jax.ShapeDtypeStruct((nl, BATCH, n), F32),
        compiler_params=_params(("arbitrary", "arbitrary")),
        name="adaln_mod",
    )(c, w, b)


def _ffn_kernel(x_ref, mod_ref, nw_ref, wgu_ref, wd_ref, fnw_ref, o_ref, *, k0, final):
    x = x_ref[...]
    shift = mod_ref[0, k0:k0 + 1, :]
    scale = mod_ref[0, k0 + 1:k0 + 2, :]
    gate = mod_ref[0, k0 + 2:k0 + 3, :]
    h = _rms_mod(x, nw_ref[...], shift, scale).astype(BF16)
    acc = None
    for c0, c1 in FF_CHUNKS:
        g = _bdot(h, wgu_ref[:, c0:c1])
        u = _bdot(h, wgu_ref[:, D_FF + c0:D_FF + c1])
        a = (_silu(g) * u).astype(BF16)
        part = _bdot(a, wd_ref[c0:c1, :])
        acc = part if acc is None else acc + part
    out = x + (FFN_HALF * gate) * acc
    if final:
        ms = jnp.mean(out * out, axis=-1, keepdims=True)
        out = out * lax.rsqrt(ms + EPS) * fnw_ref[...]
    o_ref[...] = out


def _ffn_call(x, mod, norm_w, w_gu, w_down, layer, half, final_w, final):
    per_seq = SEQ // TM_FFN
    k0 = 0 if half == 0 else 6
    return pl.pallas_call(
        functools.partial(_ffn_kernel, k0=k0, final=final),
        grid=(TOKENS // TM_FFN,),
        in_specs=[
            pl.BlockSpec((TM_FFN, D_MODEL), lambda i: (i, 0)),
            pl.BlockSpec((1, N_MOD, D_MODEL), lambda i: (i // per_seq, 0, 0)),
            pl.BlockSpec((None, None, 1, D_MODEL), lambda i: (layer, half, 0, 0)),
            _resident((None, None, D_MODEL, 2 * D_FF), lambda i: (layer, half, 0, 0)),
            _resident((None, None, D_FF, D_MODEL), lambda i: (layer, half, 0, 0)),
            pl.BlockSpec((1, D_MODEL), lambda i: (0, 0)),
        ],
        out_specs=pl.BlockSpec((TM_FFN, D_MODEL), lambda i: (i, 0)),
        out_shape=jax.ShapeDtypeStruct((TOKENS, D_MODEL), F32),
        compiler_params=_params(("arbitrary",)),
        name="swiglu_half",
    )(x, mod, norm_w, w_gu, w_down, final_w)


def _ssm_in_kernel(x_ref, mod_ref, nw_ref, wzx_ref, wdt_ref, cw_ref, cb_ref, dtb_ref,
                   z_ref, xbc_ref, dt_ref, buf_ref):
    j = pl.program_id(1)
    tl = x_ref.shape[0]
    x = x_ref[...]
    h = _rms_mod(x, nw_ref[...], mod_ref[0, 3:4, :], mod_ref[0, 4:5, :]).astype(BF16)

    @pl.when(j == 0)
    def _():
        buf_ref[...] = jnp.zeros(buf_ref.shape, F32)

    for ci in range(CONV_DIM // CONV_LANE_CHUNK):
        c0 = ci * CONV_LANE_CHUNK
        lanes = slice(c0, c0 + CONV_LANE_CHUNK)
        if ci % (CONV_DIM // D_INNER) == 0:
            z0 = ci // (CONV_DIM // D_INNER) * CONV_LANE_CHUNK
            z_ref[:, z0:z0 + CONV_LANE_CHUNK] = _bdot(h, wzx_ref[:, z0:z0 + CONV_LANE_CHUNK])
        u = _bdot(h, wzx_ref[:, D_INNER + c0:D_INNER + c0 + CONV_LANE_CHUNK])
        ext = jnp.concatenate([buf_ref[:, lanes], u], axis=0)
        buf_ref[:, lanes] = u[tl - CONV_CARRY:, :]
        acc = cw_ref[0:1, lanes] * ext
        for k in range(1, CONV_WIDTH):
            acc = cw_ref[k:k + 1, lanes] * ext + pltpu.roll(acc, 1, 0)
        xbc_ref[:, lanes] = _silu(acc[CONV_CARRY:, :] + cb_ref[:, lanes])

    dt_raw = _bdot(h, wdt_ref[...]) + dtb_ref[...]
    dt_ref[...] = jnp.maximum(dt_raw, 0.0) + jnp.log(1.0 + jnp.exp(-jnp.abs(dt_raw)))


def _ssm_in_call(x, mod, norm_w, w_zx, w_dt, conv_w, conv_b, dt_bias, layer):
    per_seq = SEQ // TM_PROJ
    row = lambda b, j: (b * per_seq + j, 0)
    return pl.pallas_call(
        _ssm_in_kernel,
        grid=(BATCH, per_seq),
        in_specs=[
            pl.BlockSpec((TM_PROJ, D_MODEL), row),
            pl.BlockSpec((1, N_MOD, D_MODEL), lambda b, j: (b, 0, 0)),
            pl.BlockSpec((None, 1, D_MODEL), lambda b, j: (layer, 0, 0)),
            _resident((None, D_MODEL, D_INNER + CONV_DIM), lambda b, j: (layer, 0, 0)),
            _resident((None, D_MODEL, LANES), lambda b, j: (layer, 0, 0)),
            pl.BlockSpec((None, CONV_WIDTH, CONV_DIM), lambda b, j: (layer, 0, 0)),
            pl.BlockSpec((None, 1, CONV_DIM), lambda b, j: (layer, 0, 0)),
            pl.BlockSpec((None, 1, LANES), lambda b, j: (layer, 0, 0)),
        ],
        out_specs=[
            pl.BlockSpec((TM_PROJ, D_INNER), row),
            pl.BlockSpec((TM_PROJ, CONV_DIM), row),
            pl.BlockSpec((TM_PROJ, LANES), row),
        ],
        out_shape=[
            jax.ShapeDtypeStruct((TOKENS, D_INNER), F32),
            jax.ShapeDtypeStruct((TOKENS, CONV_DIM), F32),
            jax.ShapeDtypeStruct((TOKENS, LANES), F32),
        ],
        scratch_shapes=[pltpu.VMEM((CONV_CARRY, CONV_DIM), F32)],
        compiler_params=_params(("arbitrary", "arbitrary")),
        name="ssm_in_conv",
    )(x, mod, norm_w, w_zx, w_dt, conv_w, conv_b, dt_bias)


def _split2(v):
    hi = v.astype(BF16)
    lo = (v - hi.astype(F32)).astype(BF16)
    return jnp.concatenate([hi, lo], axis=1)


def _ssd_kernel(xs_ref, b_ref, c_ref, dt_ref, alog_ref, dx_ref, exp_ref, y_ref, state_ref):
    j = pl.program_id(1)

    @pl.when(j == 0)
    def _():
        state_ref[...] = jnp.zeros(state_ref.shape, F32)

    head_lane = lax.broadcasted_iota(jnp.int32, (1, LANES), 1) < SSM_HEADS
    a_neg = jnp.where(head_lane, -jnp.exp(alog_ref[...]), 0.0)
    dt = dt_ref[...]
    a = dt * a_neg

    a_hi = a.astype(BF16)
    r1 = a - a_hi.astype(F32)
    a_mid = r1.astype(BF16)
    a_lo = (r1 - a_mid.astype(F32)).astype(BF16)
    row = lax.broadcasted_iota(jnp.int32, (CHUNK, CHUNK), 0)
    col = lax.broadcasted_iota(jnp.int32, (CHUNK, CHUNK), 1)
    causal = row >= col
    tri = jnp.where(causal, 1.0, 0.0).astype(BF16)
    cs3 = _bdot(tri, jnp.concatenate([a_hi, a_mid, a_lo], axis=1))
    acs = cs3[:, :LANES] + cs3[:, LANES:2 * LANES] + cs3[:, 2 * LANES:]
    acs_t = acs.T
    total = acs[CHUNK - 1:CHUNK, :]

    expand = exp_ref[...]
    dt_x = _bdot(_split2(dt), expand)
    decay_end_x = _bdot(_split2(jnp.exp(total - acs)), expand)
    decay_in_x = _bdot(_split2(jnp.exp(acs)), expand)
    chunk_decay_x = _bdot(_split2(jnp.broadcast_to(jnp.exp(total), (8, LANES))), expand)[0:1, :]

    xs = xs_ref[...]
    x_dt = xs * dt_x
    x_b = x_dt.astype(BF16)
    x_end = (x_dt * decay_end_x).astype(BF16)
    head_of_lane = lax.broadcasted_iota(jnp.int32, (CHUNK, GROUP_WIDTH), 1) // SSM_HEADDIM

    for g in range(SSM_GROUPS):
        n0 = g * SSM_STATE
        w0 = g * GROUP_WIDTH
        bg = b_ref[:, n0:n0 + SSM_STATE]
        cg = c_ref[:, n0:n0 + SSM_STATE].astype(BF16)
        cb = lax.dot_general(cg, bg.astype(BF16), (((1,), (1,)), ((), ())),
                             preferred_element_type=F32)
        xg = x_b[:, w0:w0 + GROUP_WIDTH]
        m_parts, x_parts = [], []
        for r in range(HEADS_PER_GROUP):
            hh = g * HEADS_PER_GROUP + r
            seg = acs[:, hh:hh + 1] - acs_t[hh:hh + 1, :]
            decay = jnp.exp(jnp.where(causal, seg, NEG_BIG))
            m_parts.append((cb * decay).astype(BF16))
            x_parts.append(jnp.where(head_of_lane == r, xg, jnp.zeros_like(xg)))
        y_diag = _bdot(jnp.concatenate(m_parts, axis=1), jnp.concatenate(x_parts, axis=0))

        s_prev = state_ref[g]
        y_off = _bdot(cg, s_prev.astype(BF16)) * decay_in_x[:, w0:w0 + GROUP_WIDTH]
        y_ref[:, w0:w0 + GROUP_WIDTH] = (y_diag + y_off
                                         + dx_ref[:, w0:w0 + GROUP_WIDTH] * xs[:, w0:w0 + GROUP_WIDTH])
        s_chunk = _bdot(bg.T.astype(BF16), x_end[:, w0:w0 + GROUP_WIDTH])
        state_ref[g] = s_prev * chunk_decay_x[:, w0:w0 + GROUP_WIDTH] + s_chunk


def _ssd_call(xbc, dt, a_log, d_x, expand, layer):
    per_seq = SEQ // CHUNK
    row = lambda b, j: (b * per_seq + j, 0)
    b_blk = D_INNER // (SSM_GROUPS * SSM_STATE)
    return pl.pallas_call(
        _ssd_kernel,
        grid=(BATCH, per_seq),
        in_specs=[
            pl.BlockSpec((CHUNK, D_INNER), row),
            pl.BlockSpec((CHUNK, SSM_GROUPS * SSM_STATE), lambda b, j: (b * per_seq + j, b_blk)),
            pl.BlockSpec((CHUNK, SSM_GROUPS * SSM_STATE), lambda b, j: (b * per_seq + j, b_blk + 1)),
            pl.BlockSpec((CHUNK, LANES), row),
            pl.BlockSpec((None, 1, LANES), lambda b, j: (layer, 0, 0)),
            pl.BlockSpec((None, 1, D_INNER), lambda b, j: (layer, 0, 0)),
            pl.BlockSpec((2 * LANES, D_INNER), lambda b, j: (0, 0)),
        ],
        out_specs=pl.BlockSpec((CHUNK, D_INNER), row),
        out_shape=jax.ShapeDtypeStruct((TOKENS, D_INNER), F32),
        scratch_shapes=[pltpu.VMEM((SSM_GROUPS, SSM_STATE, GROUP_WIDTH), F32)],
        compiler_params=_params(("arbitrary", "arbitrary")),
        name="ssd_chunk_scan",
    )(xbc, xbc, xbc, dt, a_log, d_x, expand)


def _ssm_out_kernel(y_ref, z_ref, x_ref, mod_ref, nw_ref, wo_ref, o_ref):
    y = y_ref[...] * _silu(z_ref[...])
    parts = []
    for g in range(SSM_GROUPS):
        blk = y[:, g * GROUP_WIDTH:(g + 1) * GROUP_WIDTH]
        ms = jnp.mean(blk * blk, axis=-1, keepdims=True)
        parts.append(blk * lax.rsqrt(ms + EPS))
    yn = (jnp.concatenate(parts, axis=1) * nw_ref[...]).astype(BF16)
    o_ref[...] = x_ref[...] + mod_ref[0, 5:6, :] * _bdot(yn, wo_ref[...])


def _ssm_out_call(y, z, x, mod, norm_w, w_out, layer):
    per_seq = SEQ // TM_PROJ
    return pl.pallas_call(
        _ssm_out_kernel,
        grid=(TOKENS // TM_PROJ,),
        in_specs=[
            pl.BlockSpec((TM_PROJ, D_INNER), lambda i: (i, 0)),
            pl.BlockSpec((TM_PROJ, D_INNER), lambda i: (i, 0)),
            pl.BlockSpec((TM_PROJ, D_MODEL), lambda i: (i, 0)),
            pl.BlockSpec((1, N_MOD, D_MODEL), lambda i: (i // per_seq, 0, 0)),
            pl.BlockSpec((None, 1, D_INNER), lambda i: (layer, 0, 0)),
            _resident((None, D_INNER, D_MODEL), lambda i: (layer, 0, 0)),
        ],
        out_specs=pl.BlockSpec((TM_PROJ, D_MODEL), lambda i: (i, 0)),
        out_shape=jax.ShapeDtypeStruct((TOKENS, D_MODEL), F32),
        compiler_params=_params(("arbitrary",)),
        name="ssm_gate_norm_out",
    )(y, z, x, mod, norm_w, w_out)


def _kv_kernel(x_ref, mod_ref, nw_ref, wk_ref, bk_ref, wvt_ref, bv_ref, k_ref, vt_ref):
    h = _rms_mod(x_ref[...], nw_ref[...], mod_ref[0, 0:1, :], mod_ref[0, 1:2, :]).astype(BF16)
    k = _bdot(h, wk_ref[...]) + bk_ref[...]
    for kh in range(KV_HEADS):
        k_ref[kh] = k[:, kh * HEAD_DIM:(kh + 1) * HEAD_DIM].astype(BF16)
    vt = lax.dot_general(wvt_ref[...], h, (((1,), (1,)), ((), ())), preferred_element_type=F32)
    vt_ref[...] = (vt + bv_ref[...]).astype(BF16)


def _kv_call(x, kv_mod, norm_w, w_k, b_k, w_vt, b_v):
    per_seq = SEQ // TM_PROJ
    return pl.pallas_call(
        _kv_kernel,
        grid=(TOKENS // TM_PROJ,),
        in_specs=[
            pl.BlockSpec((TM_PROJ, D_MODEL), lambda i: (i, 0)),
            pl.BlockSpec((1, 2, D_MODEL), lambda i: (i // per_seq, 0, 0)),
            pl.BlockSpec((1, D_MODEL), lambda i: (0, 0)),
            _resident((D_MODEL, KV_DIM), lambda i: (0, 0)),
            pl.BlockSpec((1, KV_DIM), lambda i: (0, 0)),
            _resident((KV_DIM, D_MODEL), lambda i: (0, 0)),
            pl.BlockSpec((KV_DIM, 1), lambda i: (0, 0)),
        ],
        out_specs=[
            pl.BlockSpec((KV_HEADS, TM_PROJ, HEAD_DIM), lambda i: (0, i, 0)),
            pl.BlockSpec((KV_DIM, TM_PROJ), lambda i: (0, i)),
        ],
        out_shape=[
            jax.ShapeDtypeStruct((KV_HEADS, TOKENS, HEAD_DIM), BF16),
            jax.ShapeDtypeStruct((KV_DIM, TOKENS), BF16),
        ],
        compiler_params=_params(("arbitrary",)),
        name="shared_kv",
    )(x, kv_mod, norm_w, w_k, b_k, w_vt, b_v)


SUM_ROWS = 16


def _attn_kernel(sink_ref, x_ref, mod_ref, nw_ref, wqt_ref, bq_ref, kc_ref, kp_ref, vc_ref, vp_ref,
                 wo_ref, bo_ref, o_ref, ot_ref, *, layer):
    j = pl.program_id(1)
    tl = x_ref.shape[0]
    x = x_ref[...]
    h = _rms_mod(x, nw_ref[...], mod_ref[0, 3:4, :], mod_ref[0, 4:5, :]).astype(BF16)
    qt = lax.dot_general(wqt_ref[...], h, (((1,), (1,)), ((), ())), preferred_element_type=F32)
    qt = ((qt + bq_ref[...]) * (HEAD_DIM ** -0.5)).astype(BF16)
    vt_all = jnp.concatenate([vp_ref[...], vc_ref[...]], axis=1)
    ones_rows = jnp.ones((SUM_ROWS, 2 * WINDOW), BF16)

    kpos = lax.broadcasted_iota(jnp.int32, (2 * WINDOW, Q_PER_KV * WINDOW), 0)
    qpos = lax.broadcasted_iota(jnp.int32, (2 * WINDOW, Q_PER_KV * WINDOW), 1) % WINDOW
    local = (kpos > qpos) & (kpos <= qpos + WINDOW)
    first_key = jnp.where(j == 0, WINDOW, 0)
    for n in range(tl // WINDOW):
        mask = (local & (kpos >= first_key)) if n == 0 else local
        cur = slice(n * WINDOW, (n + 1) * WINDOW)
        for kh in range(KV_HEADS):
            if n == 0:
                k_band = jnp.concatenate([kp_ref[kh], kc_ref[kh, cur, :]], axis=0)
            else:
                k_band = kc_ref[kh, (n - 1) * WINDOW:(n + 1) * WINDOW, :]
            heads = [kh * Q_PER_KV + r for r in range(Q_PER_KV)]
            q4t = jnp.concatenate([qt[hd * HEAD_DIM:(hd + 1) * HEAD_DIM, cur] for hd in heads], axis=1)
            st = jnp.where(mask, _bdot(k_band, q4t), NEG_BIG)
            sink = jnp.concatenate([jnp.full((1, WINDOW), sink_ref[layer, hd], F32) for hd in heads],
                                   axis=1)
            m = jnp.maximum(jnp.max(st, axis=0, keepdims=True), sink)
            p = jnp.exp(st - m).astype(BF16)
            v_aug = jnp.concatenate(
                [vt_all[kh * HEAD_DIM:(kh + 1) * HEAD_DIM, n * WINDOW:(n + 2) * WINDOW], ones_rows], axis=0)
            oa = _bdot(v_aug, p)
            denom = oa[HEAD_DIM:HEAD_DIM + 1, :] + jnp.exp(sink - m)
            ot = oa[:HEAD_DIM, :] / denom
            for r, hd in enumerate(heads):
                ot_ref[hd * HEAD_DIM:(hd + 1) * HEAD_DIM, cur] = ot[:, r * WINDOW:(r + 1) * WINDOW]
    o = ot_ref[...].T.astype(BF16)
    o_ref[...] = x + mod_ref[0, 5:6, :] * (_bdot(o, wo_ref[...]) + bo_ref[...])


def _attn_call(sinks, x, mod, norm_w, w_qt, b_q, k, vt, w_o, b_o, layer, att_layer):
    per_seq = SEQ // TL_ATT
    blocks_per_tile = TL_ATT // WINDOW
    row = lambda b, j: (b * per_seq + j, 0)
    prev_blk = lambda b, j: jnp.maximum((b * per_seq + j) * blocks_per_tile - 1, 0)
    hd_all = ATT_HEADS * HEAD_DIM
    return pl.pallas_call(
        functools.partial(_attn_kernel, layer=att_layer),
        grid=(BATCH, per_seq),
        in_specs=[
            pl.BlockSpec(memory_space=pltpu.SMEM),
            pl.BlockSpec((TL_ATT, D_MODEL), row),
            pl.BlockSpec((1, N_MOD, D_MODEL), lambda b, j: (b, 0, 0)),
            pl.BlockSpec((None, 1, D_MODEL), lambda b, j: (layer, 0, 0)),
            _resident((None, hd_all, D_MODEL), lambda b, j: (att_layer, 0, 0)),
            pl.BlockSpec((None, hd_all, 1), lambda b, j: (att_layer, 0, 0)),
            pl.BlockSpec((KV_HEADS, TL_ATT, HEAD_DIM), lambda b, j: (0, b * per_seq + j, 0)),
            pl.BlockSpec((KV_HEADS, WINDOW, HEAD_DIM), lambda b, j: (0, prev_blk(b, j), 0)),
            pl.BlockSpec((KV_DIM, TL_ATT), lambda b, j: (0, b * per_seq + j)),
            pl.BlockSpec((KV_DIM, WINDOW), lambda b, j: (0, prev_blk(b, j))),
            _resident((None, hd_all, D_MODEL), lambda b, j: (att_layer, 0, 0)),
            pl.BlockSpec((None, 1, D_MODEL), lambda b, j: (att_layer, 0, 0)),
        ],
        out_specs=pl.BlockSpec((TL_ATT, D_MODEL), row),
        out_shape=jax.ShapeDtypeStruct((TOKENS, D_MODEL), F32),
        scratch_shapes=[pltpu.VMEM((hd_all, TL_ATT), F32)],
        compiler_params=_params(("arbitrary", "arbitrary")),
        name="swa_sink_attention",
    )(sinks, x, mod, norm_w, w_qt, b_q, k, k, vt, vt, w_o, b_o)


def _head_expand_matrix():
    r = lax.broadcasted_iota(jnp.int32, (2 * LANES, D_INNER), 0) % LANES
    ch = lax.broadcasted_iota(jnp.int32, (2 * LANES, D_INNER), 1) // SSM_HEADDIM
    return (r == ch).astype(BF16)


def kernel(x, c, ffn_norm_w, ffn_w_gu, ffn_w_down, mod_w, mod_b, mix_norm_w, ssm_w_in, ssm_conv_w,
           ssm_conv_b, ssm_dt_bias, ssm_a_log, ssm_d, ssm_norm_w, ssm_w_out, kv_norm_w, kv_mod_w,
           kv_mod_b, w_kv, b_kv, attn_w_q, attn_b_q, attn_sinks, attn_w_o, attn_b_o, final_norm_w):
    w_gu = ffn_w_gu.astype(BF16)
    w_down = ffn_w_down.astype(BF16)
    w_zx = ssm_w_in[:, :, :D_INNER + CONV_DIM].astype(BF16)
    pad = LANES - SSM_HEADS
    w_dt = jnp.pad(ssm_w_in[:, :, D_INNER + CONV_DIM:], ((0, 0), (0, 0), (0, pad))).astype(BF16)
    dt_bias = jnp.pad(ssm_dt_bias, ((0, 0), (0, pad)))[:, None, :]
    a_log = jnp.pad(ssm_a_log, ((0, 0), (0, pad)))[:, None, :]
    d_x = jnp.repeat(ssm_d, SSM_HEADDIM, axis=1)[:, None, :]
    w_out = ssm_w_out.astype(BF16)
    w_k = w_kv[:, :KV_DIM].astype(BF16)
    w_vt = w_kv[:, KV_DIM:].T.astype(BF16)
    w_qt = jnp.swapaxes(attn_w_q, 1, 2).astype(BF16)
    w_o = attn_w_o.astype(BF16)
    expand = _head_expand_matrix()

    mod_all = _mod_call(c, mod_w, mod_b[:, None, :], (N_MOD * D_MODEL) // 4)
    kv_mod = _mod_call(c, kv_mod_w[None], kv_mod_b[None, None, :], 2 * D_MODEL)

    xf = x.reshape(TOKENS, D_MODEL)
    final_w = final_norm_w[None, :]
    k = vt = None
    for i in range(DEPTH):
        mod = mod_all[i].reshape(BATCH, N_MOD, D_MODEL)
        if i == N_A_LAYERS:
            k, vt = _kv_call(xf, kv_mod[0].reshape(BATCH, 2, D_MODEL), kv_norm_w[None, :],
                             w_k, b_kv[None, :KV_DIM], w_vt, b_kv[KV_DIM:, None])
        xf = _ffn_call(xf, mod, ffn_norm_w[:, :, None, :], w_gu, w_down, i, 0, final_w, False)
        if i < N_A_LAYERS:
            z, xbc, dt = _ssm_in_call(xf, mod, mix_norm_w[:, None, :], w_zx, w_dt, ssm_conv_w,
                                      ssm_conv_b[:, None, :], dt_bias, i)
            y = _ssd_call(xbc, dt, a_log, d_x, expand, i)
            xf = _ssm_out_call(y, z, xf, mod, ssm_norm_w[:, None, :], w_out, i)
        else:
            jj = i - N_A_LAYERS
            xf = _attn_call(attn_sinks, xf, mod, mix_norm_w[:, None, :], w_qt, attn_b_q[:, :, None],
                            k, vt, w_o, attn_b_o[:, None, :], i, jj)
        xf = _ffn_call(xf, mod, ffn_norm_w[:, :, None, :], w_gu, w_down, i, 1, final_w,
                       i == DEPTH - 1)
    return xf.reshape(BATCH, SEQ, D_MODEL)
```
